```python
import math
import jax, jax.numpy as jnp
from jax import lax
import numpy as np

D_MODEL = 1024
BATCH = 4
SEQ = 4096
DEPTH = 1
DEC_BATCH = 128
DEC_SEQ = 4
PAST_LEN = 16384
PAGE_SIZE = 128

MLA_HEADS = 8
MLA_NOPE = 64
MLA_ROPE = 32
MLA_V = 64
Q_LORA = 384
KV_LORA = 256
MLA_THETA = 10000.0
MLA_SCALE = (MLA_NOPE + MLA_ROPE) ** -0.5
DIFF_HEADS = 4
DIFF_D = 64
DIFF_V = 2 * DIFF_D
DIFF_ROT = DIFF_D // 4
ROPE_THETA = 500000.0
DIFF_SCALE = DIFF_D ** -0.5
SUBLN_EPS = 1e-5
MIX_WIDTH = MLA_HEADS * MLA_V + DIFF_HEADS * DIFF_V
D_FF = 4 * D_MODEL
NORM_EPS = 1e-6
Q_BLOCK = 128
IN_WIDTHS = (Q_LORA, KV_LORA, MLA_ROPE, DIFF_HEADS * 2 * DIFF_D, DIFF_HEADS * 2 * DIFF_D, DIFF_HEADS * DIFF_V)
IN_WIDTH = sum(IN_WIDTHS)
SPLIT_POINTS = tuple(int(s) for s in np.cumsum(IN_WIDTHS)[:-1])

kernel_name = 'hymba_mla_diffattn_decoder_step'


def rms_norm(x, g, eps=NORM_EPS):
    xf = x.astype(jnp.float32)
    y = xf * lax.rsqrt(jnp.mean(xf * xf, axis=-1, keepdims=True) + eps)
    return (y * g.astype(jnp.float32)).astype(x.dtype)


def rope(x, pos, theta):
    d = x.shape[-1]
    half = d // 2
    inv = 1.0 / jnp.power(jnp.float32(theta), jnp.arange(half, dtype=jnp.float32) / half)
    ang = pos.astype(jnp.float32)[:, None] * inv[None, :]
    shape = (1, ang.shape[0]) + (1,) * (x.ndim - 3) + (half,)
    cos = jnp.cos(ang).reshape(shape)
    sin = jnp.sin(ang).reshape(shape)
    xf = x.astype(jnp.float32)
    x1, x2 = xf[..., :half], xf[..., half:]
    return jnp.concatenate([x1 * cos - x2 * sin, x2 * cos + x1 * sin], axis=-1).astype(x.dtype)


def partial_rope(x, pos):
    return jnp.concatenate([rope(x[..., :DIFF_ROT], pos, ROPE_THETA), x[..., DIFF_ROT:]], axis=-1)


def attend_block(q, k, v, q_pos, k_pos, scale):
    s = jnp.einsum('bqghd,bkgd->bghqk', q, k, preferred_element_type=jnp.float32) * scale
    mask = k_pos[None, :] <= q_pos[:, None]
    s = jnp.where(mask, s, -jnp.inf)
    p = jax.nn.softmax(s, axis=-1)
    return jnp.einsum('bghqk,bkge->bqghe', p.astype(v.dtype), v)


def prompt_attention(q, k, v, scale):
    b, s = q.shape[:2]
    nblk = s // Q_BLOCK
    qb = jnp.moveaxis(q.reshape((b, nblk, Q_BLOCK) + q.shape[2:]), 1, 0)
    k_pos = jnp.arange(s, dtype=jnp.int32)

    def body(args):
        qi, i = args
        q_pos = i * Q_BLOCK + jnp.arange(Q_BLOCK, dtype=jnp.int32)
        return attend_block(qi, k, v, q_pos, k_pos, scale)

    out = lax.map(body, (qb, jnp.arange(nblk, dtype=jnp.int32)))
    return jnp.moveaxis(out, 0, 1).reshape((b, s) + out.shape[3:])


def sample_attention(layer, q_mla, lat, kpe, q_d, k_d, v_d, page_table, c_lat, c_kpe, c_dk, c_dv):
    t_new = q_mla.shape[1]
    past = page_table.shape[1] * PAGE_SIZE
    q_pos = past + jnp.arange(t_new, dtype=jnp.int32)
    k_pos = jnp.arange(past + t_new, dtype=jnp.int32)

    def gather(cache, pages):
        rows = cache[layer, pages]
        return rows.reshape((past,) + cache.shape[3:])

    def one_sequence(args):
        pages, qm, la, kp, qd, kd, vd = args
        lat_all = jnp.concatenate([gather(c_lat, pages), la], axis=0)[None]
        kpe_all = jnp.concatenate([gather(c_kpe, pages), kp], axis=0)[None]
        dk_all = jnp.concatenate([gather(c_dk, pages), kd], axis=0)[None]
        dv_all = jnp.concatenate([gather(c_dv, pages), vd], axis=0)[None]
        k_mla = jnp.concatenate([lat_all, kpe_all], axis=-1)[:, :, None]
        o_lat = attend_block(qm[None], k_mla, lat_all[:, :, None], q_pos, k_pos, MLA_SCALE)
        qd = qd[None]
        o1 = attend_block(qd[:, :, :, 0:1], dk_all[:, :, :, 0], dv_all, q_pos, k_pos, DIFF_SCALE)
        o2 = attend_block(qd[:, :, :, 1:2], dk_all[:, :, :, 1], dv_all, q_pos, k_pos, DIFF_SCALE)
        return o_lat[0], o1[0], o2[0]

    return lax.map(one_sequence, (page_table, q_mla, lat, kpe, q_d, k_d, v_d))


def project(hn, pos, w_in, q_norm_g, w_uq, kv_norm_g, w_uk):
    b, t, _ = hn.shape
    z = hn @ w_in
    c_q, c_kv, k_pe, dq, dk, dv = jnp.split(z, SPLIT_POINTS, axis=-1)
    c_q = rms_norm(c_q, q_norm_g)
    q = (c_q @ w_uq).reshape(b, t, MLA_HEADS, MLA_NOPE + MLA_ROPE)
    q_nope, q_pe = q[..., :MLA_NOPE], q[..., MLA_NOPE:]
    q_pe = rope(q_pe, pos, MLA_THETA)
    latent = rms_norm(c_kv, kv_norm_g)
    k_pe = rope(k_pe, pos, MLA_THETA)
    q_lat = jnp.einsum('bthn,lhn->bthl', q_nope, w_uk)
    q_mla = jnp.concatenate([q_lat, q_pe], axis=-1)[:, :, None]
    dq = partial_rope(dq.reshape(b, t, DIFF_HEADS, 2, DIFF_D), pos)
    dk = partial_rope(dk.reshape(b, t, DIFF_HEADS, 2, DIFF_D), pos)
    dv = dv.reshape(b, t, DIFF_HEADS, DIFF_V)
    return q_mla, latent, k_pe, dq, dk, dv


def mix_out(o_lat, o1, o2, lam, lam_init, w_uv, subln_g, w_out):
    b, t = o_lat.shape[:2]
    mla = jnp.einsum('bthl,lhv->bthv', o_lat[:, :, 0], w_uv)
    d = o1[:, :, :, 0] - lam.astype(o1.dtype) * o2[:, :, :, 0]
    d = rms_norm(d, subln_g, SUBLN_EPS) * (1.0 - lam_init)
    cat = jnp.concatenate([mla.reshape(b, t, -1), d.reshape(b, t, -1)], axis=-1)
    return cat @ w_out


def channel_mlp(h, g, w_up, w_down):
    u = jax.nn.relu(rms_norm(h, g) @ w_up)
    return (u * u) @ w_down


def setup_inputs(seed: int = 0) -> dict:
    key = jax.random.key(seed)
    ks = jax.random.split(key, 32)
    n_pages = PAST_LEN // PAGE_SIZE
    n_used = DEC_BATCH * n_pages
    n_pool = n_used + max(1, n_used // 4)
    f32 = jnp.float32

    def nrm(k, shape, scale=1.0):
        return jax.random.normal(k, shape, f32) * scale

    def gain(k, shape):
        return 1.0 + 0.02 * jax.random.normal(k, shape, f32)

    perm = jax.random.permutation(ks[6], n_pool)[:n_used]
    page_table = perm.reshape(DEC_BATCH, n_pages).astype(jnp.int32)
    return {
        'x_prompt': nrm(ks[0], (BATCH, SEQ, D_MODEL)),
        'x_sample': nrm(ks[1], (DEC_BATCH, DEC_SEQ, D_MODEL)),
        'cache_mla_latent': nrm(ks[2], (DEPTH, n_pool, PAGE_SIZE, KV_LORA)),
        'cache_mla_krope': nrm(ks[3], (DEPTH, n_pool, PAGE_SIZE, MLA_ROPE)),
        'cache_diff_k': nrm(ks[4], (DEPTH, n_pool, PAGE_SIZE, DIFF_HEADS, 2, DIFF_D)),
        'cache_diff_v': nrm(ks[5], (DEPTH, n_pool, PAGE_SIZE, DIFF_HEADS, DIFF_V)),
        'page_table': page_table,
        'attn_norm_g': gain(ks[7], (DEPTH, D_MODEL)),
        'w_in': nrm(ks[8], (DEPTH, D_MODEL, IN_WIDTH), D_MODEL ** -0.5),
        'q_norm_g': gain(ks[9], (DEPTH, Q_LORA)),
        'w_uq': nrm(ks[10], (DEPTH, Q_LORA, MLA_HEADS * (MLA_NOPE + MLA_ROPE)), Q_LORA ** -0.5),
        'kv_norm_g': gain(ks[11], (DEPTH, KV_LORA)),
        'w_uk': nrm(ks[12], (DEPTH, KV_LORA, MLA_HEADS, MLA_NOPE), KV_LORA ** -0.5),
        'w_uv': nrm(ks[13], (DEPTH, KV_LORA, MLA_HEADS, MLA_V), KV_LORA ** -0.5),
        'lambda_q1': nrm(ks[14], (DEPTH, DIFF_D), 0.1),
        'lambda_k1': nrm(ks[15], (DEPTH, DIFF_D), 0.1),
        'lambda_q2': nrm(ks[16], (DEPTH, DIFF_D), 0.1),
        'lambda_k2': nrm(ks[17], (DEPTH, DIFF_D), 0.1),
        'subln_g': gain(ks[18], (DEPTH, DIFF_V)),
        'w_out': nrm(ks[19], (DEPTH, MIX_WIDTH, D_MODEL), MIX_WIDTH ** -0.5),
        'mlp_norm_g': gain(ks[20], (DEPTH, D_MODEL)),
        'w_up': nrm(ks[21], (DEPTH, D_MODEL, D_FF), D_MODEL ** -0.5),
        'w_down': nrm(ks[22], (DEPTH, D_FF, D_MODEL), D_FF ** -0.5),
        'final_norm_g': gain(ks[23], (D_MODEL,)),
    }


def reference(x_prompt, x_sample, cache_mla_latent, cache_mla_krope, cache_diff_k, cache_diff_v, page_table,
              attn_norm_g, w_in, q_norm_g, w_uq, kv_norm_g, w_uk, w_uv,
              lambda_q1, lambda_k1, lambda_q2, lambda_k2, subln_g, w_out,
              mlp_norm_g, w_up, w_down, final_norm_g):
    seq = x_prompt.shape[1]
    dec_seq = x_sample.shape[1]
    past = page_table.shape[1] * PAGE_SIZE
    pos_p = jnp.arange(seq, dtype=jnp.int32)
    pos_s = past + jnp.arange(dec_seq, dtype=jnp.int32)
    hp, hs = x_prompt, x_sample
    lat_p, kpe_p, dk_p, dv_p = [], [], [], []
    lat_s, kpe_s, dk_s, dv_s = [], [], [], []
    for l in range(DEPTH):
        lam_init = 0.8 - 0.6 * math.exp(-0.3 * l)
        lam = (jnp.exp(jnp.sum(lambda_q1[l].astype(jnp.float32) * lambda_k1[l].astype(jnp.float32)))
               - jnp.exp(jnp.sum(lambda_q2[l].astype(jnp.float32) * lambda_k2[l].astype(jnp.float32)))
               + lam_init)
        qm, la, kp, q_d, k_d, v_d = project(rms_norm(hp, attn_norm_g[l]), pos_p, w_in[l], q_norm_g[l],
                                            w_uq[l], kv_norm_g[l], w_uk[l])
        k_mla = jnp.concatenate([la, kp], axis=-1)[:, :, None]
        o_lat = prompt_attention(qm, k_mla, la[:, :, None], MLA_SCALE)
        o1 = prompt_attention(q_d[:, :, :, 0:1], k_d[:, :, :, 0], v_d, DIFF_SCALE)
        o2 = prompt_attention(q_d[:, :, :, 1:2], k_d[:, :, :, 1], v_d, DIFF_SCALE)
        hp = hp + mix_out(o_lat, o1, o2, lam, lam_init, w_uv[l], subln_g[l], w_out[l])
        hp = hp + channel_mlp(hp, mlp_norm_g[l], w_up[l], w_down[l])
        lat_p.append(la); kpe_p.append(kp); dk_p.append(k_d); dv_p.append(v_d)
        qm, la, kp, q_d, k_d, v_d = project(rms_norm(hs, attn_norm_g[l]), pos_s, w_in[l], q_norm_g[l],
                                            w_uq[l], kv_norm_g[l], w_uk[l])
        o_lat, o1, o2 = sample_attention(l, qm, la, kp, q_d, k_d, v_d, page_table,
                                         cache_mla_latent, cache_mla_krope, cache_diff_k, cache_diff_v)
        hs = hs + mix_out(o_lat, o1, o2, lam, lam_init, w_uv[l], subln_g[l], w_out[l])
        hs = hs + channel_mlp(hs, mlp_norm_g[l], w_up[l], w_down[l])
        lat_s.append(la); kpe_s.append(kp); dk_s.append(k_d); dv_s.append(v_d)
    y_prompt = rms_norm(hp, final_norm_g)
    y_sample = rms_norm(hs, final_norm_g)
    return (y_prompt, y_sample,
            jnp.stack(lat_p), jnp.stack(kpe_p), jnp.stack(dk_p), jnp.stack(dv_p),
            jnp.stack(lat_s), jnp.stack(kpe_s), jnp.stack(dk_s), jnp.stack(dv_s))
```

```python
import functools
import math

import jax
import jax.numpy as jnp
from jax import lax
from jax.experimental import pallas as pl
from jax.experimental.pallas import tpu as pltpu

F32 = jnp.float32
BF16 = jnp.bfloat16

D_MODEL = 1024
PAGE = 128
MLA_HEADS = 8
MLA_NOPE = 64
MLA_ROPE = 32
MLA_V = 64
Q_LORA = 384
KV_LORA = 256
MLA_THETA = 10000.0
MLA_SCALE = (MLA_NOPE + MLA_ROPE) ** -0.5
DIFF_HEADS = 4
DIFF_D = 64
DIFF_V = 2 * DIFF_D
DIFF_ROT = DIFF_D // 4
ROPE_THETA = 500000.0
DIFF_SCALE = DIFF_D ** -0.5
SUBLN_EPS = 1e-5
NORM_EPS = 1e-6
D_FF = 4 * D_MODEL
DQK = DIFF_HEADS * 2 * DIFF_D
DVW = DIFF_HEADS * DIFF_V
LANES = 128
QK_MLA = KV_LORA + LANES
NEG = -1e30

C_CQ, C_KV, C_DQ, C_DK, C_DV, C_KPE, C_END = 0, 384, 640, 1152, 1664, 2176, 2304
VMEM_LIMIT = 56 * 1024 * 1024


def _rms(x, g, eps):
    y = x * lax.rsqrt(jnp.mean(x * x, axis=-1, keepdims=True) + eps)
    return y * g


def _proj_kernel(x_ref, tab_ref, g_ref, win_ref, qg_ref, wuq_ref, kvg_ref, wk_ref,
                 lat_ref, kpet_ref, dkt_ref, dv_ref, q_ref, kt_ref, vb_ref, qd_ref, kdt_ref, vdb_ref):
    x = x_ref[0]
    hn = _rms(x, g_ref[...], NORM_EPS).astype(BF16)
    z = jnp.dot(hn, win_ref[...], preferred_element_type=F32)

    def rope(v, base, fwd, back):
        return (v * tab_ref[base] + pltpu.roll(v, LANES - fwd, 1) * tab_ref[base + 1]
                + pltpu.roll(v, back, 1) * tab_ref[base + 2])

    rope_m = functools.partial(rope, base=0, fwd=MLA_ROPE // 2, back=MLA_ROPE // 2)
    rope_d = functools.partial(rope, base=3, fwd=DIFF_ROT // 2, back=DIFF_ROT // 2)

    cq = _rms(z[:, C_CQ:C_KV], qg_ref[...], NORM_EPS).astype(BF16)
    q = jnp.dot(cq, wuq_ref[...], preferred_element_type=F32)
    qn = q[:, :MLA_HEADS * MLA_NOPE].astype(BF16)
    for h in range(MLA_HEADS):
        pair = qn[:, (h // 2) * LANES:(h // 2 + 1) * LANES]
        q_ref[0, h, :, 0:KV_LORA] = jnp.dot(pair, wk_ref[h], preferred_element_type=F32).astype(q_ref.dtype)
        lo = MLA_HEADS * MLA_NOPE + h * LANES
        q_ref[0, h, :, KV_LORA:QK_MLA] = rope_m(q[:, lo:lo + LANES]).astype(q_ref.dtype)

    lat = _rms(z[:, C_KV:C_DQ], kvg_ref[...], NORM_EPS)
    kpe = rope_m(z[:, C_KPE:C_END])
    kpe_t = kpe.T
    lat_ref[0] = lat
    kpet_ref[0] = kpe_t[:MLA_ROPE]
    vb_ref[0] = lat.astype(BF16)
    kt_ref[0, 0, 0:KV_LORA, :] = lat.T.astype(BF16)
    kt_ref[0, 0, KV_LORA:QK_MLA, :] = kpe_t.astype(BF16)

    first_map = lax.broadcasted_iota(jnp.int32, (x.shape[0], LANES), 1) < DIFF_D
    for h in range(DIFF_HEADS):
        sl = slice(h * LANES, (h + 1) * LANES)
        dq = rope_d(z[:, C_DQ + h * LANES:C_DQ + (h + 1) * LANES]) * DIFF_SCALE
        qd_ref[0, 0, :, sl] = jnp.where(first_map, dq, 0.0).astype(BF16)
        qd_ref[0, 1, :, sl] = jnp.where(first_map, 0.0, dq).astype(BF16)
        dk_t = rope_d(z[:, C_DK + h * LANES:C_DK + (h + 1) * LANES]).T
        dkt_ref[0, sl, :] = dk_t
        kdt_ref[0, 0, sl, :] = dk_t.astype(BF16)
    dv = z[:, C_DV:C_KPE]
    dv_ref[0] = dv
    vdb_ref[0] = dv.astype(BF16)


def _proj(x, tabs, g, win, qg, wuq, kvg, wk, tm):
    b, s, _ = x.shape
    nt = s // tm
    full = lambda shape: pl.BlockSpec(shape, lambda i, j: (0,) * len(shape))
    tok = lambda w: pl.BlockSpec((1, tm, w), lambda i, j: (i, j, 0))
    feat = lambda w: pl.BlockSpec((1, w, tm), lambda i, j: (i, 0, j))
    blocked = lambda w: pl.BlockSpec((1, 1, w, tm), lambda i, j: (i, j, 0, 0))
    out_shape = (
        jax.ShapeDtypeStruct((b, s, KV_LORA), F32),
        jax.ShapeDtypeStruct((b, MLA_ROPE, s), F32),
        jax.ShapeDtypeStruct((b, DQK, s), F32),
        jax.ShapeDtypeStruct((b, s, DVW), F32),
        jax.ShapeDtypeStruct((b, MLA_HEADS, s, QK_MLA), BF16),
        jax.ShapeDtypeStruct((b, nt, QK_MLA, tm), BF16),
        jax.ShapeDtypeStruct((b, s, KV_LORA), BF16),
        jax.ShapeDtypeStruct((b, 2, s, DQK), BF16),
        jax.ShapeDtypeStruct((b, nt, DQK, tm), BF16),
        jax.ShapeDtypeStruct((b, s, DVW), BF16),
    )
    out_specs = (
        tok(KV_LORA), feat(MLA_ROPE), feat(DQK), tok(DVW),
        pl.BlockSpec((1, MLA_HEADS, tm, QK_MLA), lambda i, j: (i, 0, j, 0)),
        blocked(QK_MLA), tok(KV_LORA),
        pl.BlockSpec((1, 2, tm, DQK), lambda i, j: (i, 0, j, 0)),
        blocked(DQK), tok(DVW),
    )
    return pl.pallas_call(
        _proj_kernel,
        grid=(b, nt),
        in_specs=[
            tok(D_MODEL),
            pl.BlockSpec((6, tm, LANES), lambda i, j: (0, j, 0)),
            full(g.shape), full(win.shape), full(qg.shape), full(wuq.shape), full(kvg.shape), full(wk.shape),
        ],
        out_specs=out_specs,
        out_shape=out_shape,
        compiler_params=pltpu.CompilerParams(
            dimension_semantics=("parallel", "parallel"), vmem_limit_bytes=VMEM_LIMIT),
        name="proj",
    )(x, tabs, g, win, qg, wuq, kvg, wk)


LOG2E = math.log2(math.e)
ROW_CHUNK = 256


def _flash_scores(q, kt, c_exp, tok0, masked):
    t = jnp.dot(q, kt, preferred_element_type=F32) * c_exp
    if masked:
        row = lax.broadcasted_iota(jnp.int32, t.shape, 0) + tok0
        col = lax.broadcasted_iota(jnp.int32, t.shape, 1)
        t = jnp.where(col <= row, t, NEG)
    return t


def _flash_update(t, v, m_sc, l_sc, acc_sc, rows):
    n_keys = t.shape[1]
    m_prev = m_sc[rows]
    m_next = jnp.maximum(m_prev, jnp.max(t, axis=1, keepdims=True))
    p = jnp.exp2(t - jnp.concatenate([m_next] * (n_keys // LANES), axis=1))
    alpha = jnp.exp2(m_prev - m_next)
    l_sc[rows] = alpha * l_sc[rows] + jnp.sum(p, axis=1, keepdims=True)
    m_sc[rows] = m_next
    pv = jnp.dot(p.astype(v.dtype), v, preferred_element_type=F32)
    acc_sc[rows] = acc_sc[rows] * jnp.concatenate([alpha] * (v.shape[1] // LANES), axis=1) + pv


def _flash_kernel(q_ref, kt_ref, v_ref, *rest, tq, chunks, c_exp, n_out):
    out_refs, (m_sc, l_sc, acc_sc) = rest[:n_out], rest[n_out:]
    qi = pl.program_id(1)
    m_sc[...] = jnp.full(m_sc.shape, NEG, F32)
    l_sc[...] = jnp.zeros(l_sc.shape, F32)
    acc_sc[...] = jnp.zeros(acc_sc.shape, F32)

    def block(kb, masked):
        key0 = pl.multiple_of(kb * tq, tq)

        def n_keys(c):
            return min(tq, chunks[c][3] + ROW_CHUNK) if masked else tq

        def scores(c):
            q_idx, feat, _, tok0, _, _ = chunks[c]
            return _flash_scores(q_ref[(0,) + q_idx], kt_ref[0, kb, feat, 0:n_keys(c)], c_exp, tok0, masked)

        t = scores(0)
        for c in range(len(chunks)):
            t_next = scores(c + 1) if c + 1 < len(chunks) else None
            _flash_update(t, v_ref[0, pl.ds(key0, n_keys(c)), chunks[c][2]], m_sc, l_sc, acc_sc,
                          slice(c * ROW_CHUNK, (c + 1) * ROW_CHUNK))
            t = t_next

    def body(kb, carry):
        block(kb, False)
        return carry

    lax.fori_loop(0, qi, body, 0)
    block(qi, True)
    for c, (_, _, v_lanes, tok0, o_idx, o_lanes) in enumerate(chunks):
        rows = slice(c * ROW_CHUNK, (c + 1) * ROW_CHUNK)
        width = acc_sc.shape[1] // LANES
        o = acc_sc[rows] / jnp.concatenate([l_sc[rows]] * width, axis=1)
        out_refs[o_idx][0, tok0:tok0 + ROW_CHUNK, o_lanes] = o.astype(out_refs[o_idx].dtype)


def _flash_call(name, q, kt, v, q_spec, out_shapes, chunks, c_exp, tq, dv):
    b, nkb = kt.shape[0], kt.shape[1]
    s = nkb * tq
    n_rows = len(chunks) * ROW_CHUNK
    kernel = functools.partial(_flash_kernel, tq=tq, chunks=tuple(chunks), c_exp=c_exp, n_out=len(out_shapes))
    return pl.pallas_call(
        kernel,
        grid=(b, s // tq),
        in_specs=[
            q_spec,
            pl.BlockSpec((1,) + kt.shape[1:], lambda i, j: (i, 0, 0, 0)),
            pl.BlockSpec((1,) + v.shape[1:], lambda i, j: (i, 0, 0)),
        ],
        out_specs=tuple(pl.BlockSpec((1, tq, o.shape[2]), lambda i, j: (i, j, 0)) for o in out_shapes),
        out_shape=tuple(out_shapes),
        scratch_shapes=[
            pltpu.VMEM((n_rows, LANES), F32),
            pltpu.VMEM((n_rows, LANES), F32),
            pltpu.VMEM((n_rows, dv), F32),
        ],
        compiler_params=pltpu.CompilerParams(
            dimension_semantics=("parallel", "parallel"), vmem_limit_bytes=VMEM_LIMIT),
        name=name,
    )(q, kt, v)


def _mla_attn(q, kt, vb, tq):
    b, _, s, _ = q.shape
    chunks = [((h, slice(t0, t0 + ROW_CHUNK), slice(None)), slice(None), slice(None), t0, 0,
               slice(h * KV_LORA, (h + 1) * KV_LORA))
              for h in range(MLA_HEADS) for t0 in range(0, tq, ROW_CHUNK)]
    q_spec = pl.BlockSpec((1, MLA_HEADS, tq, QK_MLA), lambda i, j: (i, 0, j, 0))
    out = jax.ShapeDtypeStruct((b, s, MLA_HEADS * KV_LORA), BF16)
    return _flash_call("mla_attn", q, kt, vb, q_spec, [out], chunks, MLA_SCALE * LOG2E, tq, KV_LORA)[0]


def _diff_attn(qd, kdt, vdb, tq):
    b, _, s, _ = qd.shape
    chunks = [((mm, slice(t0, t0 + ROW_CHUNK), slice(h * LANES, (h + 1) * LANES)),
               slice(h * LANES, (h + 1) * LANES), slice(h * DIFF_V, (h + 1) * DIFF_V), t0, mm,
               slice(h * DIFF_V, (h + 1) * DIFF_V))
              for h in range(DIFF_HEADS) for mm in range(2) for t0 in range(0, tq, ROW_CHUNK)]
    q_spec = pl.BlockSpec((1, 2, tq, DQK), lambda i, j: (i, 0, j, 0))
    out = jax.ShapeDtypeStruct((b, s, DVW), F32)
    return _flash_call("diff_attn", qd, kdt, vdb, q_spec, [out, out], chunks, LOG2E, tq, DIFF_V)


MLA_ROWS = 32
DIFF_ROWS = 32
QROWS = MLA_ROWS + DIFF_ROWS


def _decode_kernel(pt_ref, qlat_ref, qpe_ref, qd_ref, nlat_ref, nkpe_ref, ndk_ref, ndv_ref,
                   clat, ckpe, cdk, cdv, olat_ref, odv_ref,
                   latbuf, kpebuf, dkbuf, dvbuf, tlat, tkpe, tdk, tdv, sems, m_sc, l_sc, alat, adv,
                   *, pages, n_chunks, t_new):
    s_id = pl.program_id(0)
    c_id = pl.program_id(1)
    g = s_id * n_chunks + c_id
    total = pl.num_programs(0) * n_chunks
    slot = g % 2
    keys = pages * PAGE

    def page_copies(chunk, p, sl):
        page = pt_ref[chunk * pages + p]
        cols = pl.ds(p * PAGE, PAGE)
        return (
            pltpu.make_async_copy(clat.at[page], latbuf.at[sl, pl.ds(p * PAGE, PAGE)], sems.at[0, sl]),
            pltpu.make_async_copy(ckpe.at[page], kpebuf.at[sl, :, cols], sems.at[1, sl]),
            pltpu.make_async_copy(cdk.at[page], dkbuf.at[sl, :, cols], sems.at[2, sl]),
            pltpu.make_async_copy(cdv.at[page], dvbuf.at[sl, pl.ds(p * PAGE * DIFF_HEADS, PAGE * DIFF_HEADS)],
                                  sems.at[3, sl]),
        )

    def fetch(chunk, sl):
        for p in range(pages):
            for cp in page_copies(chunk, p, sl):
                cp.start()

    @pl.when(g == 0)
    def _():
        fetch(0, 0)
        tlat[...] = jnp.zeros(tlat.shape, F32)
        tkpe[...] = jnp.zeros(tkpe.shape, F32)
        tdk[...] = jnp.zeros(tdk.shape, F32)
        tdv[...] = jnp.zeros(tdv.shape, F32)

    @pl.when(g + 1 < total)
    def _():
        fetch(g + 1, 1 - slot)

    for p in range(pages):
        for cp in page_copies(g, p, slot):
            cp.wait()

    @pl.when(c_id == 0)
    def _():
        m_sc[...] = jnp.full(m_sc.shape, NEG, F32)
        l_sc[...] = jnp.zeros(l_sc.shape, F32)
        alat[...] = jnp.zeros(alat.shape, F32)
        adv[...] = jnp.zeros(adv.shape, F32)

    def attend(lat, kpe_t, dk_t, dv_head, mask):
        s_m = (lax.dot_general(qlat_ref[0], lat, (((1,), (1,)), ((), ())), preferred_element_type=F32)
               + jnp.dot(qpe_ref[0], kpe_t, preferred_element_type=F32)) * MLA_SCALE
        s_d = [jnp.dot(qd_ref[0, h], dk_t[h * LANES:(h + 1) * LANES], preferred_element_type=F32)
               for h in range(DIFF_HEADS)]
        s = jnp.concatenate([s_m] + s_d, axis=0)
        if mask is not None:
            s = jnp.where(mask, s, NEG)
        m_old = m_sc[...]
        m_new = jnp.maximum(m_old, jnp.max(s, axis=1, keepdims=True))
        alpha = jnp.exp(m_old - m_new)
        p = jnp.exp(s - m_new)
        l_sc[...] = alpha * l_sc[...] + jnp.sum(p, axis=1, keepdims=True)
        m_sc[...] = m_new
        alat[...] = alpha[:MLA_ROWS] * alat[...] + jnp.dot(p[:MLA_ROWS], lat, preferred_element_type=F32)
        rows = 2 * t_new
        for h in range(DIFF_HEADS):
            r = slice(MLA_ROWS + h * rows, MLA_ROWS + (h + 1) * rows)
            adv[h * rows:(h + 1) * rows] = (alpha[r] * adv[h * rows:(h + 1) * rows]
                                            + jnp.dot(p[r], dv_head(h), preferred_element_type=F32))

    attend(latbuf[slot], kpebuf[slot], dkbuf[slot],
           lambda h: dvbuf[slot, pl.ds(h, keys, stride=DIFF_HEADS), :], None)

    @pl.when(c_id == n_chunks - 1)
    def _():
        tlat[0:t_new] = nlat_ref[0]
        tkpe[:, 0:t_new] = nkpe_ref[0]
        tdk[:, 0:t_new] = ndk_ref[0]
        tdv[0:t_new * DIFF_HEADS] = ndv_ref[0]
        row = lax.broadcasted_iota(jnp.int32, (QROWS, PAGE), 0)
        col = lax.broadcasted_iota(jnp.int32, (QROWS, PAGE), 1)
        tok = jnp.where(row < MLA_ROWS, row >> 3, row & (t_new - 1))
        attend(tlat[...], tkpe[...], tdk[...],
               lambda h: tdv[pl.ds(h, PAGE, stride=DIFF_HEADS), :], col <= tok)
        olat_ref[0] = alat[...] / l_sc[0:MLA_ROWS]
        odv_ref[0] = adv[...] / l_sc[MLA_ROWS:QROWS]


def _decode(page_table, qlat, qpe, qd, nlat, nkpe_t, ndk_t, ndv, clat, ckpe_t, cdk_t, cdv, pages):
    n_seq, n_pages = page_table.shape
    t_new = nlat.shape[1]
    assert t_new * MLA_HEADS == MLA_ROWS and t_new * DIFF_HEADS * 2 == DIFF_ROWS
    n_chunks = n_pages // pages
    keys = pages * PAGE
    seq_blk = lambda a: pl.BlockSpec((1,) + a.shape[1:], lambda i, j, pt: (i,) + (0,) * (a.ndim - 1))
    hbm = pl.BlockSpec(memory_space=pl.ANY)
    kernel = functools.partial(_decode_kernel, pages=pages, n_chunks=n_chunks, t_new=t_new)
    grid_spec = pltpu.PrefetchScalarGridSpec(
        num_scalar_prefetch=1,
        grid=(n_seq, n_chunks),
        in_specs=[seq_blk(qlat), seq_blk(qpe), seq_blk(qd), seq_blk(nlat), seq_blk(nkpe_t), seq_blk(ndk_t),
                  seq_blk(ndv), hbm, hbm, hbm, hbm],
        out_specs=(pl.BlockSpec((1, MLA_ROWS, KV_LORA), lambda i, j, pt: (i, 0, 0)),
                   pl.BlockSpec((1, DIFF_ROWS, DIFF_V), lambda i, j, pt: (i, 0, 0))),
        scratch_shapes=[
            pltpu.VMEM((2, keys, KV_LORA), F32),
            pltpu.VMEM((2, MLA_ROPE, keys), F32),
            pltpu.VMEM((2, DQK, keys), F32),
            pltpu.VMEM((2, keys * DIFF_HEADS, DIFF_V), F32),
            pltpu.VMEM((PAGE, KV_LORA), F32),
            pltpu.VMEM((MLA_ROPE, PAGE), F32),
            pltpu.VMEM((DQK, PAGE), F32),
            pltpu.VMEM((PAGE * DIFF_HEADS, DIFF_V), F32),
            pltpu.SemaphoreType.DMA((4, 2)),
            pltpu.VMEM((QROWS, 1), F32),
            pltpu.VMEM((QROWS, 1), F32),
            pltpu.VMEM((MLA_ROWS, KV_LORA), F32),
            pltpu.VMEM((DIFF_ROWS, DIFF_V), F32),
        ],
    )
    return pl.pallas_call(
        kernel,
        grid_spec=grid_spec,
        out_shape=(jax.ShapeDtypeStruct((n_seq, MLA_ROWS, KV_LORA), F32),
                   jax.ShapeDtypeStruct((n_seq, DIFF_ROWS, DIFF_V), F32)),
        compiler_params=pltpu.CompilerParams(
            dimension_semantics=("arbitrary", "arbitrary"), vmem_limit_bytes=VMEM_LIMIT),
        name="decode",
    )(page_table.reshape(-1), qlat, qpe, qd, nlat, nkpe_t, ndk_t, ndv, clat, ckpe_t, cdk_t, cdv)


def _post_kernel(x_ref, ol_ref, o1_ref, o2_ref, lam_ref, wuv_ref, sg_ref, wout_ref, mg_ref, wup_ref, wdn_ref,
                 fg_ref, y_ref, *, lam_init, ff_chunk):
    lq1, lk1, lq2, lk2 = lam_ref[0:1], lam_ref[1:2], lam_ref[2:3], lam_ref[3:4]
    lam = (jnp.exp(jnp.sum(lq1 * lk1, axis=-1, keepdims=True))
           - jnp.exp(jnp.sum(lq2 * lk2, axis=-1, keepdims=True)) + lam_init)
    mla = jnp.dot(ol_ref[...], wuv_ref[...], preferred_element_type=F32)
    attn = jnp.dot(mla.astype(BF16), wout_ref[0:MLA_HEADS * MLA_V], preferred_element_type=F32)
    d = o1_ref[...] - lam * o2_ref[...]
    dn = []
    for h in range(DIFF_HEADS):
        blk = _rms(d[:, h * DIFF_V:(h + 1) * DIFF_V], sg_ref[...], SUBLN_EPS) * (1.0 - lam_init)
        dn.append(blk.astype(BF16))
    attn = attn + jnp.dot(jnp.concatenate(dn, axis=1), wout_ref[MLA_HEADS * MLA_V:], preferred_element_type=F32)
    h1 = x_ref[...] + attn
    hn = _rms(h1, mg_ref[...], NORM_EPS).astype(BF16)
    acc = jnp.zeros_like(h1)
    for f in range(D_FF // ff_chunk):
        u = jnp.maximum(jnp.dot(hn, wup_ref[:, f * ff_chunk:(f + 1) * ff_chunk], preferred_element_type=F32), 0.0)
        acc = acc + jnp.dot((u * u).astype(BF16), wdn_ref[f * ff_chunk:(f + 1) * ff_chunk],
                            preferred_element_type=F32)
    y_ref[...] = _rms(h1 + acc, fg_ref[...], NORM_EPS)


def _post(x, ol, o1, o2, lam_vecs, wuv, sg, wout, mg, wup, wdn, fg, lam_init, tm):
    n = x.shape[0]
    tok = lambda w: pl.BlockSpec((tm, w), lambda i: (i, 0))
    const = lambda a: pl.BlockSpec(a.shape, lambda i: (0,) * a.ndim, pipeline_mode=pl.Buffered(1))
    return pl.pallas_call(
        functools.partial(_post_kernel, lam_init=lam_init, ff_chunk=1024),
        grid=(n // tm,),
        in_specs=[tok(D_MODEL), tok(MLA_HEADS * KV_LORA), tok(DVW), tok(DVW),
                  const(lam_vecs), const(wuv), const(sg), const(wout), const(mg), const(wup), const(wdn), const(fg)],
        out_specs=tok(D_MODEL),
        out_shape=jax.ShapeDtypeStruct((n, D_MODEL), F32),
        compiler_params=pltpu.CompilerParams(dimension_semantics=("parallel",), vmem_limit_bytes=VMEM_LIMIT),
        name="post",
    )(x, ol, o1, o2, lam_vecs, wuv, sg, wout, mg, wup, wdn, fg)


def _rope_tables(pos):
    lane = jnp.arange(LANES)

    def tables(theta, period, rot):
        half = rot // 2
        inv = 1.0 / jnp.power(jnp.float32(theta), jnp.arange(half, dtype=F32) / half)
        ang = pos.astype(F32)[:, None] * inv[None, :]
        cos, sin = jnp.cos(ang), jnp.sin(ang)
        j = lane % period
        f = j % half
        c = jnp.where(j < rot, cos[:, f], 1.0)
        sa = jnp.where(j < half, -sin[:, f], 0.0)
        sb = jnp.where((j >= half) & (j < rot), sin[:, f], 0.0)
        return [c, sa, sb]

    return jnp.stack(tables(MLA_THETA, MLA_ROPE, MLA_ROPE) + tables(ROPE_THETA, DIFF_D, DIFF_ROT)).astype(F32)


def _prep_weights(w_in, w_uq, w_uk, w_uv):
    cq, ckv, kpe, dq, dk, dv = jnp.split(w_in, [384, 640, 672, 1184, 1696], axis=1)
    win = jnp.concatenate([cq, ckv, dq, dk, dv, jnp.pad(kpe, ((0, 0), (0, LANES - MLA_ROPE)))], axis=1)
    wq = w_uq.reshape(Q_LORA, MLA_HEADS, MLA_NOPE + MLA_ROPE)
    wq_rope = jnp.pad(wq[:, :, MLA_NOPE:], ((0, 0), (0, 0), (0, LANES - MLA_ROPE)))
    wuq = jnp.concatenate([wq[:, :, :MLA_NOPE].reshape(Q_LORA, -1), wq_rope.reshape(Q_LORA, -1)], axis=1)
    wk_t = jnp.transpose(w_uk, (1, 2, 0))
    wk = jnp.stack([jnp.pad(wk_t[h], (((h % 2) * MLA_NOPE, (1 - h % 2) * MLA_NOPE), (0, 0)))
                    for h in range(MLA_HEADS)])
    wuv = jnp.zeros((MLA_HEADS * KV_LORA, MLA_HEADS * MLA_V), w_uv.dtype)
    for h in range(MLA_HEADS):
        wuv = wuv.at[h * KV_LORA:(h + 1) * KV_LORA, h * MLA_V:(h + 1) * MLA_V].set(w_uv[:, h, :])
    return win.astype(BF16), wuq.astype(BF16), wk.astype(BF16), wuv.astype(BF16)


def kernel(x_prompt, x_sample, cache_mla_latent, cache_mla_krope, cache_diff_k, cache_diff_v, page_table, attn_norm_g, w_in, q_norm_g, w_uq, kv_norm_g, w_uk, w_uv, lambda_q1, lambda_k1, lambda_q2, lambda_k2, subln_g, w_out, mlp_norm_g, w_up, w_down, final_norm_g):
    depth = w_in.shape[0]
    assert depth == 1, "single-layer trunk"
    b, s, _ = x_prompt.shape
    n_seq, t_new, _ = x_sample.shape
    n_pool = cache_mla_latent.shape[1]
    past = page_table.shape[1] * PAGE
    lam_init = 0.8 - 0.6 * math.exp(-0.3 * 0)

    win, wuq, wk, wuv = _prep_weights(w_in[0], w_uq[0], w_uk[0], w_uv[0])
    row = lambda v: v.reshape(1, -1).astype(F32)
    g_attn, g_q, g_kv = row(attn_norm_g[0]), row(q_norm_g[0]), row(kv_norm_g[0])
    lam_vecs = jnp.stack([lambda_q1[0], lambda_k1[0], lambda_q2[0], lambda_k2[0]]).astype(F32)
    post_w = (lam_vecs, wuv, row(subln_g[0]), w_out[0].astype(BF16), row(mlp_norm_g[0]),
              w_up[0].astype(BF16), w_down[0].astype(BF16), row(final_norm_g))

    tabs_p = _rope_tables(jnp.arange(s, dtype=jnp.int32))
    tm = min(512, s)
    lat_p, kpet_p, dkt_p, dv_p, q_p, kt_p, vb_p, qd_p, kdt_p, vdb_p = _proj(
        x_prompt, tabs_p, g_attn, win, g_q, wuq, g_kv, wk, tm)
    ol_p = _mla_attn(q_p, kt_p, vb_p, tm)
    o1_p, o2_p = _diff_attn(qd_p, kdt_p, vdb_p, tm)
    n_p = b * s
    y_p = _post(x_prompt.reshape(n_p, D_MODEL), ol_p.reshape(n_p, -1), o1_p.reshape(n_p, DVW),
                o2_p.reshape(n_p, DVW), *post_w, lam_init, min(512, n_p))

    n_s = n_seq * t_new
    pos_s = past + jnp.arange(t_new, dtype=jnp.int32)
    tabs_s = _rope_tables(jnp.tile(pos_s, n_seq))
    lat_s, kpet_s, dkt_s, dv_s, q_s, _, _, qd_s, _, _ = _proj(
        x_sample.reshape(1, n_s, D_MODEL), tabs_s, g_attn, win, g_q, wuq, g_kv, wk, n_s)
    kpet_s = kpet_s.reshape(MLA_ROPE, n_seq, t_new)
    dkt_s = dkt_s.reshape(DQK, n_seq, t_new)
    qm = q_s[0].astype(F32).reshape(MLA_HEADS, n_seq, t_new, QK_MLA).transpose(1, 2, 0, 3)
    qm = qm.reshape(n_seq, MLA_ROWS, QK_MLA)
    qlat, qpe = qm[:, :, :KV_LORA], qm[:, :, KV_LORA:KV_LORA + MLA_ROPE]
    qd = qd_s[0].astype(F32).reshape(2, n_seq, t_new, DIFF_HEADS, LANES).transpose(1, 3, 0, 2, 4)
    qd = qd.reshape(n_seq, DIFF_HEADS, 2 * t_new, LANES)
    ol_s, od_s = _decode(
        page_table, qlat, qpe, qd,
        lat_s.reshape(n_seq, t_new, KV_LORA),
        jnp.swapaxes(kpet_s, 0, 1),
        jnp.swapaxes(dkt_s, 0, 1),
        dv_s.reshape(n_seq, t_new * DIFF_HEADS, DIFF_V),
        cache_mla_latent.reshape(n_pool, PAGE, KV_LORA),
        jnp.swapaxes(cache_mla_krope.reshape(n_pool, PAGE, MLA_ROPE), 1, 2),
        jnp.transpose(cache_diff_k.reshape(n_pool, PAGE, DQK), (0, 2, 1)),
        cache_diff_v.reshape(n_pool, PAGE * DIFF_HEADS, DIFF_V), pages=16)
    od = od_s.reshape(n_seq, DIFF_HEADS, 2, t_new, DIFF_V).transpose(2, 0, 3, 1, 4).reshape(2, n_s, DVW)
    y_s = _post(x_sample.reshape(n_s, D_MODEL), ol_s.reshape(n_s, -1).astype(BF16), od[0], od[1],
                *post_w, lam_init, n_s)

    kpe_p = jnp.swapaxes(kpet_p, 1, 2)
    dk_p = jnp.transpose(dkt_p.reshape(b, DIFF_HEADS, 2, DIFF_D, s), (0, 4, 1, 2, 3))
    kpe_s = jnp.transpose(kpet_s, (1, 2, 0))
    dk_s = jnp.transpose(dkt_s.reshape(DIFF_HEADS, 2, DIFF_D, n_seq, t_new), (3, 4, 0, 1, 2))
    return (
        y_p.reshape(b, s, D_MODEL), y_s.reshape(n_seq, t_new, D_MODEL),
        lat_p.reshape(depth, b, s, KV_LORA), kpe_p.reshape(depth, b, s, MLA_ROPE),
        dk_p.reshape(depth, b, s, DIFF_HEADS, 2, DIFF_D), dv_p.reshape(depth, b, s, DIFF_HEADS, DIFF_V),
        lat_s.reshape(depth, n_seq, t_new, KV_LORA), kpe_s.reshape(depth, n_seq, t_new, MLA_ROPE),
        dk_s.reshape(depth, n_seq, t_new, DIFF_HEADS, 2, DIFF_D),
        dv_s.reshape(depth, n_seq, t_new, DIFF_HEADS, DIFF_V),
    )
```

```python
import functools
import math

import jax
import jax.numpy as jnp
from jax import lax
from jax.experimental import pallas as pl
from jax.experimental.pallas import tpu as pltpu

F32 = jnp.float32
BF16 = jnp.bfloat16

D_MODEL = 1024
PAGE = 128
MLA_HEADS = 8
MLA_NOPE = 64
MLA_ROPE = 32
MLA_V = 64
Q_LORA = 384
KV_LORA = 256
MLA_THETA = 10000.0
MLA_SCALE = (MLA_NOPE + MLA_ROPE) ** -0.5
DIFF_HEADS = 4
DIFF_D = 64
DIFF_V = 2 * DIFF_D
DIFF_ROT = DIFF_D // 4
ROPE_THETA = 500000.0
DIFF_SCALE = DIFF_D ** -0.5
SUBLN_EPS = 1e-5
NORM_EPS = 1e-6
D_FF = 4 * D_MODEL
DQK = DIFF_HEADS * 2 * DIFF_D
DVW = DIFF_HEADS * DIFF_V
LANES = 128
QK_MLA = KV_LORA + LANES
NEG = -1e30

C_CQ, C_KV, C_DQ, C_DK, C_DV, C_KPE, C_END = 0, 384, 640, 1152, 1664, 2176, 2304
VMEM_LIMIT = 56 * 1024 * 1024


def _rms(x, g, eps):
    y = x * lax.rsqrt(jnp.mean(x * x, axis=-1, keepdims=True) + eps)
    return y * g


def _proj_kernel(x_ref, tab_ref, g_ref, win_ref, qg_ref, wuq_ref, kvg_ref, wk_ref,
                 lat_ref, kpet_ref, dkt_ref, dv_ref, q_ref, qd_ref, *prompt_refs, absorb):
    x = x_ref[0]
    hn = _rms(x, g_ref[...], NORM_EPS).astype(BF16)
    z = jnp.dot(hn, win_ref[...], preferred_element_type=F32)

    def rope(v, base, fwd, back):
        return (v * tab_ref[base] + pltpu.roll(v, LANES - fwd, 1) * tab_ref[base + 1]
                + pltpu.roll(v, back, 1) * tab_ref[base + 2])

    rope_m = functools.partial(rope, base=0, fwd=MLA_ROPE // 2, back=MLA_ROPE // 2)
    rope_d = functools.partial(rope, base=3, fwd=DIFF_ROT // 2, back=DIFF_ROT // 2)

    cq = _rms(z[:, C_CQ:C_KV], qg_ref[...], NORM_EPS).astype(BF16)
    q = jnp.dot(cq, wuq_ref[...], preferred_element_type=F32)
    for h in range(MLA_HEADS):
        qh = q[:, h * LANES:(h + 1) * LANES]
        if absorb:
            q_ref[0, h, :, 0:KV_LORA] = jnp.dot(qh.astype(BF16), wk_ref[h],
                                                preferred_element_type=F32).astype(q_ref.dtype)
            q_ref[0, h, :, KV_LORA:QK_MLA] = rope_m(qh).astype(q_ref.dtype)
        else:
            q_ref[0, h] = rope_m(qh).astype(q_ref.dtype)

    lat = _rms(z[:, C_KV:C_DQ], kvg_ref[...], NORM_EPS)
    kpe = rope_m(z[:, C_KPE:C_END])
    lat_ref[0] = lat
    kpet_ref[0] = kpe.T[MLA_NOPE:MLA_NOPE + MLA_ROPE]
    if not absorb:
        kt_ref, vb_ref, kdt_ref, vdb_ref = prompt_refs
        lat_b = lat.astype(BF16)
        vb_ref[0] = lat_b
        kn = jnp.dot(lat_b, wk_ref[...], preferred_element_type=F32)
        for h in range(MLA_HEADS):
            sl = slice(h * LANES, (h + 1) * LANES)
            kt_ref[0, 0, sl, :] = (kn[:, sl] + kpe).T.astype(BF16)

    first_map = lax.broadcasted_iota(jnp.int32, (x.shape[0], LANES), 1) < DIFF_D
    for h in range(DIFF_HEADS):
        sl = slice(h * LANES, (h + 1) * LANES)
        dq = rope_d(z[:, C_DQ + h * LANES:C_DQ + (h + 1) * LANES]) * DIFF_SCALE
        qd_ref[0, 0, :, sl] = jnp.where(first_map, dq, 0.0).astype(BF16)
        qd_ref[0, 1, :, sl] = jnp.where(first_map, 0.0, dq).astype(BF16)
        dk_t = rope_d(z[:, C_DK + h * LANES:C_DK + (h + 1) * LANES]).T
        dkt_ref[0, sl, :] = dk_t
        if not absorb:
            kdt_ref[0, 0, sl, :] = dk_t.astype(BF16)
    dv = z[:, C_DV:C_KPE]
    for h in range(DIFF_HEADS):
        dv_ref[0, pl.ds(h, x.shape[0], stride=DIFF_HEADS), :] = dv[:, h * DIFF_V:(h + 1) * DIFF_V]
    if not absorb:
        vdb_ref[0] = dv.astype(BF16)


def _proj(x, tabs, g, win, qg, wuq, kvg, wk, tm, absorb):
    b, s, _ = x.shape
    nt = s // tm
    q_width = QK_MLA if absorb else LANES
    full = lambda shape: pl.BlockSpec(shape, lambda i, j: (0,) * len(shape))
    tok = lambda w: pl.BlockSpec((1, tm, w), lambda i, j: (i, j, 0))
    feat = lambda w: pl.BlockSpec((1, w, tm), lambda i, j: (i, 0, j))
    blocked = lambda w: pl.BlockSpec((1, 1, w, tm), lambda i, j: (i, j, 0, 0))
    out_shape = [
        jax.ShapeDtypeStruct((b, s, KV_LORA), F32),
        jax.ShapeDtypeStruct((b, MLA_ROPE, s), F32),
        jax.ShapeDtypeStruct((b, DQK, s), F32),
        jax.ShapeDtypeStruct((b, s * DIFF_HEADS, DIFF_V), F32),
        jax.ShapeDtypeStruct((b, MLA_HEADS, s, q_width), BF16),
        jax.ShapeDtypeStruct((b, 2, s, DQK), BF16),
    ]
    out_specs = [
        tok(KV_LORA), feat(MLA_ROPE), feat(DQK),
        pl.BlockSpec((1, tm * DIFF_HEADS, DIFF_V), lambda i, j: (i, j, 0)),
        pl.BlockSpec((1, MLA_HEADS, tm, q_width), lambda i, j: (i, 0, j, 0)),
        pl.BlockSpec((1, 2, tm, DQK), lambda i, j: (i, 0, j, 0)),
    ]
    if not absorb:
        out_shape += [
            jax.ShapeDtypeStruct((b, nt, MLA_HEADS * LANES, tm), BF16),
            jax.ShapeDtypeStruct((b, s, KV_LORA), BF16),
            jax.ShapeDtypeStruct((b, nt, DQK, tm), BF16),
            jax.ShapeDtypeStruct((b, s, DVW), BF16),
        ]
        out_specs += [blocked(MLA_HEADS * LANES), tok(KV_LORA), blocked(DQK), tok(DVW)]
    return pl.pallas_call(
        functools.partial(_proj_kernel, absorb=absorb),
        grid=(b, nt),
        in_specs=[
            tok(D_MODEL),
            pl.BlockSpec((6, tm, LANES), lambda i, j: (0, j, 0)),
            full(g.shape), full(win.shape), full(qg.shape), full(wuq.shape), full(kvg.shape), full(wk.shape),
        ],
        out_specs=out_specs,
        out_shape=out_shape,
        compiler_params=pltpu.CompilerParams(
            dimension_semantics=("parallel", "parallel"), vmem_limit_bytes=VMEM_LIMIT),
        name="proj",
    )(x, tabs, g, win, qg, wuq, kvg, wk)


LOG2E = math.log2(math.e)
ROW_CHUNK = 256


def _flash_scores(q, kt, c_exp, tok0, masked):
    t = jnp.dot(q, kt, preferred_element_type=F32) * c_exp
    if masked:
        row = lax.broadcasted_iota(jnp.int32, t.shape, 0) + tok0
        col = lax.broadcasted_iota(jnp.int32, t.shape, 1)
        t = jnp.where(col <= row, t, NEG)
    return t


def _flash_update(t, v, m_sc, l_sc, acc_sc, rows):
    n_keys = t.shape[1]
    m_prev = m_sc[rows]
    m_next = jnp.maximum(m_prev, jnp.max(t, axis=1, keepdims=True))
    p = jnp.exp2(t - jnp.concatenate([m_next] * (n_keys // LANES), axis=1))
    alpha = jnp.exp2(m_prev - m_next)
    l_sc[rows] = alpha * l_sc[rows] + jnp.sum(p, axis=1, keepdims=True)
    m_sc[rows] = m_next
    pv = jnp.dot(p.astype(v.dtype), v, preferred_element_type=F32)
    acc_sc[rows] = acc_sc[rows] * jnp.concatenate([alpha] * (v.shape[1] // LANES), axis=1) + pv


def _flash_kernel(q_ref, kt_ref, v_ref, *rest, tq, chunks, c_exp, n_out):
    out_refs, (m_sc, l_sc, acc_sc) = rest[:n_out], rest[n_out:]
    qi = pl.program_id(1)
    m_sc[...] = jnp.full(m_sc.shape, NEG, F32)
    l_sc[...] = jnp.zeros(l_sc.shape, F32)
    acc_sc[...] = jnp.zeros(acc_sc.shape, F32)

    def block(kb, masked):
        key0 = pl.multiple_of(kb * tq, tq)

        def n_keys(c):
            return min(tq, chunks[c][3] + ROW_CHUNK) if masked else tq

        def scores(c):
            q_idx, feat, _, tok0, _, _ = chunks[c]
            return _flash_scores(q_ref[(0,) + q_idx], kt_ref[0, kb, feat, 0:n_keys(c)], c_exp, tok0, masked)

        t = scores(0)
        for c in range(len(chunks)):
            t_next = scores(c + 1) if c + 1 < len(chunks) else None
            _flash_update(t, v_ref[0, pl.ds(key0, n_keys(c)), chunks[c][2]], m_sc, l_sc, acc_sc,
                          slice(c * ROW_CHUNK, (c + 1) * ROW_CHUNK))
            t = t_next

    def body(kb, carry):
        block(kb, False)
        return carry

    lax.fori_loop(0, qi, body, 0)
    block(qi, True)
    for c, (_, _, v_lanes, tok0, o_idx, o_lanes) in enumerate(chunks):
        rows = slice(c * ROW_CHUNK, (c + 1) * ROW_CHUNK)
        width = acc_sc.shape[1] // LANES
        o = acc_sc[rows] / jnp.concatenate([l_sc[rows]] * width, axis=1)
        out_refs[o_idx][0, tok0:tok0 + ROW_CHUNK, o_lanes] = o.astype(out_refs[o_idx].dtype)


def _flash_call(name, q, kt, v, q_spec, out_shapes, chunks, c_exp, tq, dv):
    b, nkb = kt.shape[0], kt.shape[1]
    s = nkb * tq
    n_rows = len(chunks) * ROW_CHUNK
    kernel = functools.partial(_flash_kernel, tq=tq, chunks=tuple(chunks), c_exp=c_exp, n_out=len(out_shapes))
    return pl.pallas_call(
        kernel,
        grid=(b, s // tq),
        in_specs=[
            q_spec,
            pl.BlockSpec((1,) + kt.shape[1:], lambda i, j: (i, 0, 0, 0)),
            pl.BlockSpec((1,) + v.shape[1:], lambda i, j: (i, 0, 0)),
        ],
        out_specs=tuple(pl.BlockSpec((1, tq, o.shape[2]), lambda i, j: (i, j, 0)) for o in out_shapes),
        out_shape=tuple(out_shapes),
        scratch_shapes=[
            pltpu.VMEM((n_rows, LANES), F32),
            pltpu.VMEM((n_rows, LANES), F32),
            pltpu.VMEM((n_rows, dv), F32),
        ],
        compiler_params=pltpu.CompilerParams(
            dimension_semantics=("parallel", "parallel"), vmem_limit_bytes=VMEM_LIMIT),
        name=name,
    )(q, kt, v)


def _mla_attn(q, kt, vb, tq):
    b, _, s, _ = q.shape
    chunks = [((h, slice(t0, t0 + ROW_CHUNK), slice(None)), slice(h * LANES, (h + 1) * LANES), slice(None), t0, 0,
               slice(h * KV_LORA, (h + 1) * KV_LORA))
              for h in range(MLA_HEADS) for t0 in range(0, tq, ROW_CHUNK)]
    q_spec = pl.BlockSpec((1, MLA_HEADS, tq, LANES), lambda i, j: (i, 0, j, 0))
    out = jax.ShapeDtypeStruct((b, s, MLA_HEADS * KV_LORA), BF16)
    return _flash_call("mla_attn", q, kt, vb, q_spec, [out], chunks, MLA_SCALE * LOG2E, tq, KV_LORA)[0]


def _diff_attn(qd, kdt, vdb, tq):
    b, _, s, _ = qd.shape
    chunks = [((mm, slice(t0, t0 + ROW_CHUNK), slice(h * LANES, (h + 1) * LANES)),
               slice(h * LANES, (h + 1) * LANES), slice(h * DIFF_V, (h + 1) * DIFF_V), t0, mm,
               slice(h * DIFF_V, (h + 1) * DIFF_V))
              for h in range(DIFF_HEADS) for mm in range(2) for t0 in range(0, tq, ROW_CHUNK)]
    q_spec = pl.BlockSpec((1, 2, tq, DQK), lambda i, j: (i, 0, j, 0))
    out = jax.ShapeDtypeStruct((b, s, DVW), F32)
    return _flash_call("diff_attn", qd, kdt, vdb, q_spec, [out, out], chunks, LOG2E, tq, DIFF_V)


MLA_ROWS = 32
DIFF_ROWS = 32
QROWS = MLA_ROWS + DIFF_ROWS


def _decode_kernel(pt_ref, qlat_ref, qpe_ref, qd_ref, nlat_ref, nkpe_ref, ndk_ref, ndv_ref,
                   clat, ckpe, cdk, cdv, olat_ref, odv_ref,
                   latbuf, kpebuf, dkbuf, dvbuf, tlat, tkpe, tdk, tdv, sems, m_sc, l_sc, alat, adv,
                   *, pages, n_chunks, t_new):
    s_id = pl.program_id(0)
    c_id = pl.program_id(1)
    g = s_id * n_chunks + c_id
    total = pl.num_programs(0) * n_chunks
    slot = g % 2
    keys = pages * PAGE

    def page_copies(chunk, p, sl):
        page = pt_ref[chunk * pages + p]
        cols = pl.ds(p * PAGE, PAGE)
        return (
            pltpu.make_async_copy(clat.at[page], latbuf.at[sl, pl.ds(p * PAGE, PAGE)], sems.at[0, sl]),
            pltpu.make_async_copy(ckpe.at[page], kpebuf.at[sl, :, cols], sems.at[1, sl]),
            pltpu.make_async_copy(cdk.at[page], dkbuf.at[sl, :, cols], sems.at[2, sl]),
            pltpu.make_async_copy(cdv.at[page], dvbuf.at[sl, pl.ds(p * PAGE * DIFF_HEADS, PAGE * DIFF_HEADS)],
                                  sems.at[3, sl]),
        )

    def fetch(chunk, sl):
        for p in range(pages):
            for cp in page_copies(chunk, p, sl):
                cp.start()

    @pl.when(g == 0)
    def _():
        fetch(0, 0)
        tlat[...] = jnp.zeros(tlat.shape, F32)
        tkpe[...] = jnp.zeros(tkpe.shape, F32)
        tdk[...] = jnp.zeros(tdk.shape, F32)
        tdv[...] = jnp.zeros(tdv.shape, F32)

    @pl.when(g + 1 < total)
    def _():
        fetch(g + 1, 1 - slot)

    for p in range(pages):
        for cp in page_copies(g, p, slot):
            cp.wait()

    @pl.when(c_id == 0)
    def _():
        m_sc[...] = jnp.full(m_sc.shape, NEG, F32)
        l_sc[...] = jnp.zeros(l_sc.shape, F32)
        alat[...] = jnp.zeros(alat.shape, F32)
        adv[...] = jnp.zeros(adv.shape, F32)

    def attend(lat, kpe_t, dk_t, dv_head, mask):
        s_m = (lax.dot_general(qlat_ref[0], lat, (((1,), (1,)), ((), ())), preferred_element_type=F32)
               + jnp.dot(qpe_ref[0], kpe_t, preferred_element_type=F32)) * MLA_SCALE
        s_d = [jnp.dot(qd_ref[0, h], dk_t[h * LANES:(h + 1) * LANES], preferred_element_type=F32)
               for h in range(DIFF_HEADS)]
        s = jnp.concatenate([s_m] + s_d, axis=0)
        if mask is not None:
            s = jnp.where(mask, s, NEG)
        m_old = m_sc[...]
        m_new = jnp.maximum(m_old, jnp.max(s, axis=1, keepdims=True))
        alpha = jnp.exp(m_old - m_new)
        p = jnp.exp(s - m_new)
        l_sc[...] = alpha * l_sc[...] + jnp.sum(p, axis=1, keepdims=True)
        m_sc[...] = m_new
        alat[...] = alpha[:MLA_ROWS] * alat[...] + jnp.dot(p[:MLA_ROWS], lat, preferred_element_type=F32)
        rows = 2 * t_new
        for h in range(DIFF_HEADS):
            r = slice(MLA_ROWS + h * rows, MLA_ROWS + (h + 1) * rows)
            adv[h * rows:(h + 1) * rows] = (alpha[r] * adv[h * rows:(h + 1) * rows]
                                            + jnp.dot(p[r], dv_head(h), preferred_element_type=F32))

    attend(latbuf[slot], kpebuf[slot], dkbuf[slot],
           lambda h: dvbuf[slot, pl.ds(h, keys, stride=DIFF_HEADS), :], None)

    @pl.when(c_id == n_chunks - 1)
    def _():
        tlat[0:t_new] = nlat_ref[0]
        tkpe[:, 0:t_new] = nkpe_ref[0]
        tdk[:, 0:t_new] = ndk_ref[0]
        tdv[0:t_new * DIFF_HEADS] = ndv_ref[0]
        row = lax.broadcasted_iota(jnp.int32, (QROWS, PAGE), 0)
        col = lax.broadcasted_iota(jnp.int32, (QROWS, PAGE), 1)
        tok = jnp.where(row < MLA_ROWS, row >> 3, row & (t_new - 1))
        attend(tlat[...], tkpe[...], tdk[...],
               lambda h: tdv[pl.ds(h, PAGE, stride=DIFF_HEADS), :], col <= tok)
        olat_ref[0] = alat[...] / l_sc[0:MLA_ROWS]
        odv_ref[0] = adv[...] / l_sc[MLA_ROWS:QROWS]


def _decode(page_table, qlat, qpe, qd, nlat, nkpe_t, ndk_t, ndv, clat, ckpe_t, cdk_t, cdv, pages):
    n_seq, n_pages = page_table.shape
    t_new = nlat.shape[1]
    assert t_new * MLA_HEADS == MLA_ROWS and t_new * DIFF_HEADS * 2 == DIFF_ROWS
    n_chunks = n_pages // pages
    keys = pages * PAGE
    seq_blk = lambda a: pl.BlockSpec((1,) + a.shape[1:], lambda i, j, pt: (i,) + (0,) * (a.ndim - 1))
    hbm = pl.BlockSpec(memory_space=pl.ANY)
    kernel = functools.partial(_decode_kernel, pages=pages, n_chunks=n_chunks, t_new=t_new)
    grid_spec = pltpu.PrefetchScalarGridSpec(
        num_scalar_prefetch=1,
        grid=(n_seq, n_chunks),
        in_specs=[seq_blk(qlat), seq_blk(qpe), seq_blk(qd), seq_blk(nlat), seq_blk(nkpe_t), seq_blk(ndk_t),
                  seq_blk(ndv), hbm, hbm, hbm, hbm],
        out_specs=(pl.BlockSpec((1, MLA_ROWS, KV_LORA), lambda i, j, pt: (i, 0, 0)),
                   pl.BlockSpec((1, DIFF_ROWS, DIFF_V), lambda i, j, pt: (i, 0, 0))),
        scratch_shapes=[
            pltpu.VMEM((2, keys, KV_LORA), F32),
            pltpu.VMEM((2, MLA_ROPE, keys), F32),
            pltpu.VMEM((2, DQK, keys), F32),
            pltpu.VMEM((2, keys * DIFF_HEADS, DIFF_V), F32),
            pltpu.VMEM((PAGE, KV_LORA), F32),
            pltpu.VMEM((MLA_ROPE, PAGE), F32),
            pltpu.VMEM((DQK, PAGE), F32),
            pltpu.VMEM((PAGE * DIFF_HEADS, DIFF_V), F32),
            pltpu.SemaphoreType.DMA((4, 2)),
            pltpu.VMEM((QROWS, 1), F32),
            pltpu.VMEM((QROWS, 1), F32),
            pltpu.VMEM((MLA_ROWS, KV_LORA), F32),
            pltpu.VMEM((DIFF_ROWS, DIFF_V), F32),
        ],
    )
    return pl.pallas_call(
        kernel,
        grid_spec=grid_spec,
        out_shape=(jax.ShapeDtypeStruct((n_seq, MLA_ROWS, KV_LORA), F32),
                   jax.ShapeDtypeStruct((n_seq, DIFF_ROWS, DIFF_V), F32)),
        compiler_params=pltpu.CompilerParams(
            dimension_semantics=("arbitrary", "arbitrary"), vmem_limit_bytes=VMEM_LIMIT),
        name="decode",
    )(page_table.reshape(-1), qlat, qpe, qd, nlat, nkpe_t, ndk_t, ndv, clat, ckpe_t, cdk_t, cdv)


def _post_kernel(x_ref, ol_ref, o1_ref, o2_ref, lam_ref, wuv_ref, sg_ref, wout_ref, mg_ref, wup_ref, wdn_ref,
                 fg_ref, y_ref, *, lam_init, ff_chunk):
    lq1, lk1, lq2, lk2 = lam_ref[0:1], lam_ref[1:2], lam_ref[2:3], lam_ref[3:4]
    lam = (jnp.exp(jnp.sum(lq1 * lk1, axis=-1, keepdims=True))
           - jnp.exp(jnp.sum(lq2 * lk2, axis=-1, keepdims=True)) + lam_init)
    mla = jnp.dot(ol_ref[...], wuv_ref[...], preferred_element_type=F32)
    attn = jnp.dot(mla.astype(BF16), wout_ref[0:MLA_HEADS * MLA_V], preferred_element_type=F32)
    d = o1_ref[...] - lam * o2_ref[...]
    dn = []
    for h in range(DIFF_HEADS):
        blk = _rms(d[:, h * DIFF_V:(h + 1) * DIFF_V], sg_ref[...], SUBLN_EPS) * (1.0 - lam_init)
        dn.append(blk.astype(BF16))
    attn = attn + jnp.dot(jnp.concatenate(dn, axis=1), wout_ref[MLA_HEADS * MLA_V:], preferred_element_type=F32)
    h1 = x_ref[...] + attn
    hn = _rms(h1, mg_ref[...], NORM_EPS).astype(BF16)
    acc = jnp.zeros_like(h1)
    for f in range(D_FF // ff_chunk):
        u = jnp.maximum(jnp.dot(hn, wup_ref[:, f * ff_chunk:(f + 1) * ff_chunk], preferred_element_type=F32), 0.0)
        acc = acc + jnp.dot((u * u).astype(BF16), wdn_ref[f * ff_chunk:(f + 1) * ff_chunk],
                            preferred_element_type=F32)
    y_ref[...] = _rms(h1 + acc, fg_ref[...], NORM_EPS)


def _post(x, ol, o1, o2, lam_vecs, wuv, sg, wout, mg, wup, wdn, fg, lam_init, tm):
    n = x.shape[0]
    tok = lambda w: pl.BlockSpec((tm, w), lambda i: (i, 0))
    const = lambda a: pl.BlockSpec(a.shape, lambda i: (0,) * a.ndim, pipeline_mode=pl.Buffered(1))
    return pl.pallas_call(
        functools.partial(_post_kernel, lam_init=lam_init, ff_chunk=1024),
        grid=(n // tm,),
        in_specs=[tok(D_MODEL), tok(MLA_HEADS * KV_LORA), tok(DVW), tok(DVW),
                  const(lam_vecs), const(wuv), const(sg), const(wout), const(mg), const(wup), const(wdn), const(fg)],
        out_specs=tok(D_MODEL),
        out_shape=jax.ShapeDtypeStruct((n, D_MODEL), F32),
        compiler_params=pltpu.CompilerParams(dimension_semantics=("parallel",), vmem_limit_bytes=VMEM_LIMIT),
        name="post",
    )(x, ol, o1, o2, lam_vecs, wuv, sg, wout, mg, wup, wdn, fg)


def _rope_tables(pos):
    lane = jnp.arange(LANES)

    def tables(theta, period, start, rot):
        half = rot // 2
        inv = 1.0 / jnp.power(jnp.float32(theta), jnp.arange(half, dtype=F32) / half)
        ang = pos.astype(F32)[:, None] * inv[None, :]
        cos, sin = jnp.cos(ang), jnp.sin(ang)
        j = lane % period - start
        f = j % half
        c = jnp.where((j >= 0) & (j < rot), cos[:, f], 1.0)
        sa = jnp.where((j >= 0) & (j < half), -sin[:, f], 0.0)
        sb = jnp.where((j >= half) & (j < rot), sin[:, f], 0.0)
        return [c, sa, sb]

    return jnp.stack(tables(MLA_THETA, LANES, MLA_NOPE, MLA_ROPE)
                     + tables(ROPE_THETA, DIFF_D, 0, DIFF_ROT)).astype(F32)


def _prep_weights(w_in, w_uq, w_uk, w_uv):
    cq, ckv, kpe, dq, dk, dv = jnp.split(w_in, [384, 640, 672, 1184, 1696], axis=1)
    tail = LANES - MLA_NOPE - MLA_ROPE
    win = jnp.concatenate([cq, ckv, dq, dk, dv, jnp.pad(kpe, ((0, 0), (MLA_NOPE, tail)))], axis=1)
    wq = w_uq.reshape(Q_LORA, MLA_HEADS, MLA_NOPE + MLA_ROPE)
    wuq = jnp.pad(wq, ((0, 0), (0, 0), (0, tail))).reshape(Q_LORA, MLA_HEADS * LANES)
    wk_abs = jnp.pad(jnp.transpose(w_uk, (1, 2, 0)), ((0, 0), (0, LANES - MLA_NOPE), (0, 0)))
    wk_up = jnp.pad(w_uk, ((0, 0), (0, 0), (0, LANES - MLA_NOPE))).reshape(KV_LORA, MLA_HEADS * LANES)
    wuv = jnp.zeros((MLA_HEADS * KV_LORA, MLA_HEADS * MLA_V), w_uv.dtype)
    for h in range(MLA_HEADS):
        wuv = wuv.at[h * KV_LORA:(h + 1) * KV_LORA, h * MLA_V:(h + 1) * MLA_V].set(w_uv[:, h, :])
    return win.astype(BF16), wuq.astype(BF16), wk_abs.astype(BF16), wk_up.astype(BF16), wuv.astype(BF16)


def kernel(x_prompt, x_sample, cache_mla_latent, cache_mla_krope, cache_diff_k, cache_diff_v, page_table, attn_norm_g, w_in, q_norm_g, w_uq, kv_norm_g, w_uk, w_uv, lambda_q1, lambda_k1, lambda_q2, lambda_k2, subln_g, w_out, mlp_norm_g, w_up, w_down, final_norm_g):
    depth = w_in.shape[0]
    assert depth == 1, "single-layer trunk"
    b, s, _ = x_prompt.shape
    n_seq, t_new, _ = x_sample.shape
    n_pool = cache_mla_latent.shape[1]
    past = page_table.shape[1] * PAGE
    lam_init = 0.8 - 0.6 * math.exp(-0.3 * 0)

    win, wuq, wk_abs, wk_up, wuv = _prep_weights(w_in[0], w_uq[0], w_uk[0], w_uv[0])
    row = lambda v: v.reshape(1, -1).astype(F32)
    g_attn, g_q, g_kv = row(attn_norm_g[0]), row(q_norm_g[0]), row(kv_norm_g[0])
    lam_vecs = jnp.stack([lambda_q1[0], lambda_k1[0], lambda_q2[0], lambda_k2[0]]).astype(F32)
    post_w = (lam_vecs, wuv, row(subln_g[0]), w_out[0].astype(BF16), row(mlp_norm_g[0]),
              w_up[0].astype(BF16), w_down[0].astype(BF16), row(final_norm_g))

    tabs_p = _rope_tables(jnp.arange(s, dtype=jnp.int32))
    tm = min(512, s)
    lat_p, kpet_p, dkt_p, dv_p, q_p, qd_p, kt_p, vb_p, kdt_p, vdb_p = _proj(
        x_prompt, tabs_p, g_attn, win, g_q, wuq, g_kv, wk_up, tm, absorb=False)
    ol_p = _mla_attn(q_p, kt_p, vb_p, tm)
    o1_p, o2_p = _diff_attn(qd_p, kdt_p, vdb_p, tm)
    n_p = b * s
    y_p = _post(x_prompt.reshape(n_p, D_MODEL), ol_p.reshape(n_p, -1), o1_p.reshape(n_p, DVW),
                o2_p.reshape(n_p, DVW), *post_w, lam_init, min(512, n_p))

    n_s = n_seq * t_new
    pos_s = past + jnp.arange(t_new, dtype=jnp.int32)
    tabs_s = _rope_tables(jnp.tile(pos_s, n_seq))
    lat_s, kpet_s, dkt_s, dv_s, q_s, qd_s = _proj(
        x_sample.reshape(1, n_s, D_MODEL), tabs_s, g_attn, win, g_q, wuq, g_kv, wk_abs, n_s, absorb=True)
    kpet_s = kpet_s.reshape(MLA_ROPE, n_seq, t_new)
    dkt_s = dkt_s.reshape(DQK, n_seq, t_new)
    qm = q_s[0].astype(F32).reshape(MLA_HEADS, n_seq, t_new, QK_MLA).transpose(1, 2, 0, 3)
    qm = qm.reshape(n_seq, MLA_ROWS, QK_MLA)
    qlat = qm[:, :, :KV_LORA]
    qpe = qm[:, :, KV_LORA + MLA_NOPE:KV_LORA + MLA_NOPE + MLA_ROPE]
    qd = qd_s[0].astype(F32).reshape(2, n_seq, t_new, DIFF_HEADS, LANES).transpose(1, 3, 0, 2, 4)
    qd = qd.reshape(n_seq, DIFF_HEADS, 2 * t_new, LANES)
    ol_s, od_s = _decode(
        page_table, qlat, qpe, qd,
        lat_s.reshape(n_seq, t_new, KV_LORA),
        jnp.swapaxes(kpet_s, 0, 1),
        jnp.swapaxes(dkt_s, 0, 1),
        dv_s.reshape(n_seq, t_new * DIFF_HEADS, DIFF_V),
        cache_mla_latent.reshape(n_pool, PAGE, KV_LORA),
        jnp.swapaxes(cache_mla_krope.reshape(n_pool, PAGE, MLA_ROPE), 1, 2),
        jnp.transpose(cache_diff_k.reshape(n_pool, PAGE, DQK), (0, 2, 1)),
        cache_diff_v.reshape(n_pool, PAGE * DIFF_HEADS, DIFF_V), pages=32)
    od = od_s.reshape(n_seq, DIFF_HEADS, 2, t_new, DIFF_V).transpose(2, 0, 3, 1, 4).reshape(2, n_s, DVW)
    y_s = _post(x_sample.reshape(n_s, D_MODEL), ol_s.reshape(n_s, -1).astype(BF16), od[0], od[1],
                *post_w, lam_init, n_s)

    kpe_p = jnp.swapaxes(kpet_p, 1, 2)
    dk_p = jnp.transpose(dkt_p.reshape(b, DIFF_HEADS, 2, DIFF_D, s), (0, 4, 1, 2, 3))
    kpe_s = jnp.transpose(kpet_s, (1, 2, 0))
    dk_s = jnp.transpose(dkt_s.reshape(DIFF_HEADS, 2, DIFF_D, n_seq, t_new), (3, 4, 0, 1, 2))
    return (
        y_p.reshape(b, s, D_MODEL), y_s.reshape(n_seq, t_new, D_MODEL),
        lat_p.reshape(depth, b, s, KV_LORA), kpe_p.reshape(depth, b, s, MLA_ROPE),
        dk_p.reshape(depth, b, s, DIFF_HEADS, 2, DIFF_D), dv_p.reshape(depth, b, s, DIFF_HEADS, DIFF_V),
        lat_s.reshape(depth, n_seq, t_new, KV_LORA), kpe_s.reshape(depth, n_seq, t_new, MLA_ROPE),
        dk_s.reshape(depth, n_seq, t_new, DIFF_HEADS, 2, DIFF_D),
        dv_s.reshape(depth, n_seq, t_new, DIFF_HEADS, DIFF_V),
    )
```
